```python
import jax, jax.numpy as jnp
from jax import lax
import numpy as np

D_MODEL = 1024
BATCH = 8
SEQ = 2048
DEPTH = 1

CHUNK = 64
GLA_HEADS = 4
GLA_KEY_DIM = D_MODEL // 2
GLA_VAL_DIM = D_MODEL
GLA_DK = GLA_KEY_DIM // GLA_HEADS
GLA_DV = GLA_VAL_DIM // GLA_HEADS
GLA_GATE_RANK = 16
GLA_GATE_TAU = 16.0
CONV_WIDTH = D_MODEL
CONV_KERNEL = 31
FFN_HIDDEN = -(-8 * D_MODEL // (3 * 256)) * 256
NORM_EPS = 1e-6
IN_SIZES = (GLA_KEY_DIM, GLA_KEY_DIM, GLA_VAL_DIM, GLA_VAL_DIM, GLA_GATE_RANK,
            2 * CONV_WIDTH, D_MODEL, D_MODEL)
IN_COLS = GLA_KEY_DIM * 2 + GLA_VAL_DIM * 2 + GLA_GATE_RANK + 2 * CONV_WIDTH + 2 * D_MODEL

kernel_name = "hybrid_gla_conformer_conv_gated_block"


def rms_norm(x, g):
    xf = x.astype(jnp.float32)
    y = xf * lax.rsqrt(jnp.mean(xf * xf, axis=-1, keepdims=True) + NORM_EPS)
    return (y * g.astype(jnp.float32)).astype(x.dtype)


def layer_norm(x, g, b):
    xf = x.astype(jnp.float32)
    mu = jnp.mean(xf, axis=-1, keepdims=True)
    xc = xf - mu
    var = jnp.mean(xc * xc, axis=-1, keepdims=True)
    y = xc * lax.rsqrt(var + NORM_EPS)
    return (y * g.astype(jnp.float32) + b.astype(jnp.float32)).astype(x.dtype)


def gla_chunked(q, k, v, log_a):
    B, S = q.shape[0], q.shape[1]
    N = S // CHUNK

    def to_chunks(t, d):
        return t.astype(jnp.float32).reshape(B, N, CHUNK, GLA_HEADS, d).transpose(0, 3, 1, 2, 4)

    qc = to_chunks(q, GLA_DK) * (GLA_DK ** -0.5)
    kc = to_chunks(k, GLA_DK)
    vc = to_chunks(v, GLA_DV)
    gc = to_chunks(log_a, GLA_DK)

    L = jnp.cumsum(gc, axis=3)
    L_last = L[:, :, :, -1:, :]
    q_dec = qc * jnp.exp(L)
    k_dec = kc * jnp.exp(-L)

    scores = jnp.einsum('bhnid,bhnjd->bhnij', q_dec, k_dec)
    causal = jnp.tril(jnp.ones((CHUNK, CHUNK), dtype=bool))
    scores = jnp.where(causal, scores, 0.0)
    o_intra = jnp.einsum('bhnij,bhnje->bhnie', scores, vc)

    k_to_end = kc * jnp.exp(L_last - L)
    U = jnp.einsum('bhncd,bhnce->nbhde', k_to_end, vc)
    chunk_decay = jnp.exp(L_last[:, :, :, 0, :]).transpose(2, 0, 1, 3)

    def step(state, inp):
        u, a = inp
        return a[..., None] * state + u, state

    init = jnp.zeros((B, GLA_HEADS, GLA_DK, GLA_DV), jnp.float32)
    _, S_prev = lax.scan(step, init, (U, chunk_decay))
    o_inter = jnp.einsum('bhncd,nbhde->bhnce', q_dec, S_prev)

    o = o_intra + o_inter
    return o.transpose(0, 2, 3, 1, 4).reshape(B, S, GLA_HEADS, GLA_DV)


def gla_branch(q, k, v, r, a_lr, w_alpha_up, b_alpha, gla_norm, w_gla_out):
    B, S = q.shape[0], q.shape[1]
    z = (a_lr @ w_alpha_up + b_alpha).astype(jnp.float32)
    log_a = jax.nn.log_sigmoid(z) / GLA_GATE_TAU
    shp_k = (B, S, GLA_HEADS, GLA_DK)
    o = gla_chunked(q.reshape(shp_k), k.reshape(shp_k),
                    v.reshape(B, S, GLA_HEADS, GLA_DV), log_a.reshape(shp_k))
    o = o * lax.rsqrt(jnp.mean(o * o, axis=-1, keepdims=True) + NORM_EPS)
    o = o * gla_norm.astype(jnp.float32).reshape(GLA_HEADS, GLA_DV)
    o = o.reshape(B, S, GLA_VAL_DIM).astype(q.dtype) * jax.nn.silu(r)
    return o @ w_gla_out


def conv_branch(u_glu, conv_w, conv_b, conv_ln_g, conv_ln_b, w_conv_out):
    a, b = jnp.split(u_glu, 2, axis=-1)
    z = a * jax.nn.sigmoid(b)
    z = jnp.pad(z, ((0, 0), (CONV_KERNEL - 1, 0), (0, 0)))
    z = lax.conv_general_dilated(z, conv_w[:, None, :], window_strides=(1,), padding='VALID',
                                 dimension_numbers=('NWC', 'WIO', 'NWC'),
                                 feature_group_count=CONV_WIDTH) + conv_b
    z = jax.nn.silu(layer_norm(z, conv_ln_g, conv_ln_b))
    return z @ w_conv_out


def setup_inputs(seed: int = 0) -> dict:
    key = jax.random.key(seed)
    ks = jax.random.split(key, 24)
    f32 = jnp.float32

    def w(k, shape, fan_in):
        return jax.random.normal(k, shape, f32) * (fan_in ** -0.5)

    def gain(k, shape):
        return 1.0 + 0.02 * jax.random.normal(k, shape, f32)

    def bias(k, shape, s=0.02):
        return s * jax.random.normal(k, shape, f32)

    L = DEPTH
    return {
        "x": jax.random.normal(ks[0], (BATCH, SEQ, D_MODEL), f32),
        "norm_mix_pre": gain(ks[1], (L, D_MODEL)),
        "w_in": w(ks[2], (L, D_MODEL, IN_COLS), D_MODEL),
        "w_alpha_up": w(ks[3], (L, GLA_GATE_RANK, GLA_KEY_DIM), GLA_GATE_RANK),
        "b_alpha": bias(ks[4], (L, GLA_KEY_DIM), 0.1),
        "gla_norm": gain(ks[5], (L, GLA_VAL_DIM)),
        "w_gla_out": w(ks[6], (L, GLA_VAL_DIM, D_MODEL), GLA_VAL_DIM),
        "conv_w": w(ks[7], (L, CONV_KERNEL, CONV_WIDTH), CONV_KERNEL),
        "conv_b": bias(ks[8], (L, CONV_WIDTH)),
        "conv_ln_g": gain(ks[9], (L, CONV_WIDTH)),
        "conv_ln_b": bias(ks[10], (L, CONV_WIDTH)),
        "w_conv_out": w(ks[11], (L, CONV_WIDTH, D_MODEL), CONV_WIDTH),
        "w_out": w(ks[12], (L, D_MODEL, D_MODEL), D_MODEL),
        "norm_mix_post": gain(ks[13], (L, D_MODEL)),
        "norm_ffn_pre": gain(ks[14], (L, D_MODEL)),
        "w_ffn_in": w(ks[15], (L, D_MODEL, 2 * FFN_HIDDEN), D_MODEL),
        "w_ffn_out": w(ks[16], (L, FFN_HIDDEN, D_MODEL), FFN_HIDDEN),
        "norm_ffn_post": gain(ks[17], (L, D_MODEL)),
    }


def reference(x, norm_mix_pre, w_in, w_alpha_up, b_alpha, gla_norm, w_gla_out,
              conv_w, conv_b, conv_ln_g, conv_ln_b, w_conv_out, w_out, norm_mix_post,
              norm_ffn_pre, w_ffn_in, w_ffn_out, norm_ffn_post):
    split_idx = [int(i) for i in np.cumsum(IN_SIZES)[:-1]]
    for l in range(DEPTH):
        h = rms_norm(x, norm_mix_pre[l])
        proj = h @ w_in[l]
        q, k, v, r, a_lr, u_glu, g_gla, g_conv = jnp.split(proj, split_idx, axis=-1)
        y_gla = gla_branch(q, k, v, r, a_lr, w_alpha_up[l], b_alpha[l], gla_norm[l], w_gla_out[l])
        y_conv = conv_branch(u_glu, conv_w[l], conv_b[l], conv_ln_g[l], conv_ln_b[l], w_conv_out[l])
        merged = jax.nn.sigmoid(g_gla) * y_gla + jax.nn.sigmoid(g_conv) * y_conv
        x = x + rms_norm(merged @ w_out[l], norm_mix_post[l])
        h = rms_norm(x, norm_ffn_pre[l])
        gate, up = jnp.split(h @ w_ffn_in[l], 2, axis=-1)
        f = (jax.nn.silu(gate) * up) @ w_ffn_out[l]
        x = x + rms_norm(f, norm_ffn_post[l])
    return x
```

```python
import functools

import jax
import jax.numpy as jnp
from jax import lax
from jax.experimental import pallas as pl
from jax.experimental.pallas import tpu as pltpu

F32 = jnp.float32
BF16 = jnp.bfloat16

D_MODEL = 1024
HEADS = 4
KEY_DIM = 512
VAL_DIM = 1024
DK = KEY_DIM // HEADS
DV = VAL_DIM // HEADS
RANK = 16
TAU = 16.0
CONV_C = 1024
CONV_K = 31
CHUNK = 64
CHUNK_SHIFT = CHUNK.bit_length() - 1
FFN_H = 2816
EPS = 1e-6

LANES = 128
RANK_PAD = LANES
HALO = 32
CUM_BLOCK = 256

OFF_Q = 0
OFF_K = OFF_Q + KEY_DIM
OFF_V = OFF_K + KEY_DIM
OFF_R = OFF_V + VAL_DIM
OFF_A = OFF_R + VAL_DIM
OFF_U = OFF_A + RANK_PAD
OFF_GG = OFF_U + 2 * CONV_C
OFF_GC = OFF_GG + D_MODEL
IN_COLS_PACKED = OFF_GC + D_MODEL

SEQ_TILE = 512
FFN_TILE = 512
FFN_CHUNK = 256
CONV_ROWS = 128
VMEM_LIMIT = 56 * 1024 * 1024


def _dot(a, b):
    return jnp.dot(a, b, preferred_element_type=F32)


def _dot_nt(a, b):
    return lax.dot_general(a, b, (((1,), (1,)), ((), ())), preferred_element_type=F32)


def _dot_tn(a, b):
    return lax.dot_general(a, b, (((0,), (0,)), ((), ())), preferred_element_type=F32)


def _rms(x, g):
    return x * lax.rsqrt(jnp.mean(x * x, axis=-1, keepdims=True) + EPS) * g


def _sigmoid(x):
    return 1.0 / (1.0 + jnp.exp(-x))


def _mixer_kernel(x_ref, gpre_ref, win_ref, wau_ref, balpha_ref, gnorm_ref, wgo_ref,
                  convw_ref, convb_ref, lng_ref, lnb_ref, wco_ref, wout_ref, gpost_ref,
                  o_ref, state_ref, zbuf_ref, l_ref, oacc_ref, conv_ref, *, ts):
    s = pl.program_id(1)

    @pl.when(s == 0)
    def _():
        state_ref[...] = jnp.zeros_like(state_ref)
        zbuf_ref[:, 0:HALO, :] = jnp.zeros((CONV_C // LANES, HALO, LANES), F32)

    x = x_ref[...]
    h = _rms(x, gpre_ref[...]).astype(BF16)

    def proj(lo, n):
        return _dot(h, win_ref[:, lo:lo + n])

    a_lr = proj(OFF_A, RANK_PAD).astype(BF16)
    z = _dot(a_lr, wau_ref[...]) + balpha_ref[...]
    log_a = (jnp.minimum(z, 0.0) - jnp.log(1.0 + jnp.exp(-jnp.abs(z)))) * (1.0 / TAU)

    row = lax.broadcasted_iota(jnp.int32, (CUM_BLOCK, CUM_BLOCK), 0)
    col = lax.broadcasted_iota(jnp.int32, (CUM_BLOCK, CUM_BLOCK), 1)
    same_chunk = lax.shift_right_logical(row, CHUNK_SHIFT) == lax.shift_right_logical(col, CHUNK_SHIFT)
    tri = ((row >= col) & same_chunk).astype(BF16)
    for blk in range(ts // CUM_BLOCK):
        la = log_a[blk * CUM_BLOCK:(blk + 1) * CUM_BLOCK]
        hi = la.astype(BF16)
        lo = (la - hi.astype(F32)).astype(BF16)
        l_ref[blk * CUM_BLOCK:(blk + 1) * CUM_BLOCK, :] = _dot(tri, hi) + _dot(tri, lo)
    L = l_ref[...]

    q = proj(OFF_Q, KEY_DIM) * (DK ** -0.5)
    k = proj(OFF_K, KEY_DIM)
    v = proj(OFF_V, VAL_DIM).astype(BF16)
    q_dec = (q * jnp.exp(L)).astype(BF16)
    k_dec = (k * jnp.exp(-L)).astype(BF16)

    ci = lax.broadcasted_iota(jnp.int32, (CHUNK, CHUNK), 0)
    cj = lax.broadcasted_iota(jnp.int32, (CHUNK, CHUNK), 1)
    causal = ci >= cj

    states = [state_ref[hd] for hd in range(HEADS)]
    for c in range(ts // CHUNK):
        r0 = c * CHUNK
        l_last = l_ref[pl.ds(r0 + CHUNK - 1, 1), :]
        k_end = (k[r0:r0 + CHUNK] * jnp.exp(l_last - L[r0:r0 + CHUNK])).astype(BF16)
        decay = jnp.exp(l_last)
        for hd in range(HEADS):
            kl = slice(hd * DK, (hd + 1) * DK)
            vl = slice(hd * DV, (hd + 1) * DV)
            qd = q_dec[r0:r0 + CHUNK, kl]
            vv = v[r0:r0 + CHUNK, vl]
            sc = _dot_nt(qd, k_dec[r0:r0 + CHUNK, kl])
            sc = jnp.where(causal, sc, 0.0).astype(BF16)
            st = states[hd]
            oacc_ref[r0:r0 + CHUNK, vl] = _dot(sc, vv) + _dot_nt(qd, st.astype(BF16))
            states[hd] = st * decay[:, kl] + _dot_tn(vv, k_end[:, kl])
    for hd in range(HEADS):
        state_ref[hd] = states[hd]

    r = proj(OFF_R, VAL_DIM)
    gate = r * _sigmoid(r)
    o = oacc_ref[...]
    gnorm = gnorm_ref[...]
    normed = []
    for hd in range(HEADS):
        vl = slice(hd * DV, (hd + 1) * DV)
        oh = o[:, vl]
        ms = jnp.mean(oh * oh, axis=-1, keepdims=True)
        normed.append(oh * lax.rsqrt(ms + EPS) * gnorm[:, vl])
    o_n = jnp.concatenate(normed, axis=-1)
    y_gla = _dot((o_n * gate).astype(BF16), wgo_ref[...])

    u = proj(OFF_U, 2 * CONV_C)
    zc = u[:, :CONV_C] * _sigmoid(u[:, CONV_C:])
    n_lb = CONV_C // LANES
    for lb in range(n_lb):
        zbuf_ref[lb, HALO:HALO + ts, :] = zc[:, lb * LANES:(lb + 1) * LANES]
    for lb in range(n_lb):
        ll = slice(lb * LANES, (lb + 1) * LANES)
        for rb in range(ts // CONV_ROWS):
            base = rb * CONV_ROWS + HALO - (CONV_K - 1)
            acc = jnp.zeros((CONV_ROWS, LANES), F32)
            for j in range(CONV_K):
                acc = acc + convw_ref[j:j + 1, ll] * zbuf_ref[lb, pl.ds(base + j, CONV_ROWS), :]
            conv_ref[rb * CONV_ROWS:(rb + 1) * CONV_ROWS, ll] = acc
    zbuf_ref[:, 0:HALO, :] = zbuf_ref[:, ts:ts + HALO, :]
    cv = conv_ref[...] + convb_ref[...]
    mu = jnp.mean(cv, axis=-1, keepdims=True)
    xc = cv - mu
    var = jnp.mean(xc * xc, axis=-1, keepdims=True)
    ln = xc * lax.rsqrt(var + EPS) * lng_ref[...] + lnb_ref[...]
    y_conv = _dot((ln * _sigmoid(ln)).astype(BF16), wco_ref[...])

    merged = _sigmoid(proj(OFF_GG, D_MODEL)) * y_gla + _sigmoid(proj(OFF_GC, D_MODEL)) * y_conv
    m = _dot(merged.astype(BF16), wout_ref[...])
    o_ref[...] = x + _rms(m, gpost_ref[...])


def _ffn_kernel(x_ref, gpre_ref, wfi_ref, wfo_ref, gpost_ref, o_ref, act_ref):
    x = x_ref[...]
    h = _rms(x, gpre_ref[...]).astype(BF16)
    for c in range(FFN_H // FFN_CHUNK):
        lo = c * FFN_CHUNK
        g = _dot(h, wfi_ref[:, lo:lo + FFN_CHUNK])
        up = _dot(h, wfi_ref[:, FFN_H + lo:FFN_H + lo + FFN_CHUNK])
        act_ref[:, lo:lo + FFN_CHUNK] = (g * _sigmoid(g) * up).astype(BF16)
    f = _dot(act_ref[...], wfo_ref[...])
    o_ref[...] = x + _rms(f, gpost_ref[...])


def _resident(shape):
    nd = len(shape)
    return pl.BlockSpec(shape, lambda *_: (0,) * nd, pipeline_mode=pl.Buffered(1))


def _mixer(x, gpre, win, wau, balpha, gnorm, wgo, convw, convb, lng, lnb, wco, wout, gpost):
    B, S, D = x.shape
    ts = SEQ_TILE
    assert S % ts == 0 and ts % CUM_BLOCK == 0 and ts % CONV_ROWS == 0
    x_spec = pl.BlockSpec((None, ts, D), lambda b, s: (b, s, 0))
    params = (gpre, win, wau, balpha, gnorm, wgo, convw, convb, lng, lnb, wco, wout, gpost)
    return pl.pallas_call(
        functools.partial(_mixer_kernel, ts=ts),
        grid=(B, S // ts),
        in_specs=[x_spec] + [_resident(p.shape) for p in params],
        out_specs=x_spec,
        out_shape=jax.ShapeDtypeStruct(x.shape, x.dtype),
        scratch_shapes=[
            pltpu.VMEM((HEADS, DV, DK), F32),
            pltpu.VMEM((CONV_C // LANES, HALO + ts, LANES), F32),
            pltpu.VMEM((ts, KEY_DIM), F32),
            pltpu.VMEM((ts, VAL_DIM), F32),
            pltpu.VMEM((ts, CONV_C), F32),
        ],
        compiler_params=pltpu.CompilerParams(
            dimension_semantics=("arbitrary", "arbitrary"),
            vmem_limit_bytes=VMEM_LIMIT),
        name="mixer",
    )(x, *params)


def _ffn(x, gpre, wfi, wfo, gpost):
    T, D = x.shape
    tm = FFN_TILE
    assert T % tm == 0
    x_spec = pl.BlockSpec((tm, D), lambda i: (i, 0))
    params = (gpre, wfi, wfo, gpost)
    return pl.pallas_call(
        _ffn_kernel,
        grid=(T // tm,),
        in_specs=[x_spec] + [_resident(p.shape) for p in params],
        out_specs=x_spec,
        out_shape=jax.ShapeDtypeStruct(x.shape, x.dtype),
        scratch_shapes=[pltpu.VMEM((tm, FFN_H), BF16)],
        compiler_params=pltpu.CompilerParams(
            dimension_semantics=("arbitrary",),
            vmem_limit_bytes=VMEM_LIMIT),
        name="ffn",
    )(x, *params)


def _pack_w_in(w_in):
    a0 = 2 * KEY_DIM + 2 * VAL_DIM
    a_pad = jnp.pad(w_in[:, a0:a0 + RANK], ((0, 0), (0, RANK_PAD - RANK)))
    return jnp.concatenate([w_in[:, :a0], a_pad, w_in[:, a0 + RANK:]], axis=1).astype(BF16)


def kernel(x, norm_mix_pre, w_in, w_alpha_up, b_alpha, gla_norm, w_gla_out, conv_w, conv_b,
           conv_ln_g, conv_ln_b, w_conv_out, w_out, norm_mix_post, norm_ffn_pre, w_ffn_in,
           w_ffn_out, norm_ffn_post):
    B, S, D = x.shape
    depth = w_in.shape[0]
    row = lambda p: p.reshape(1, -1)
    for l in range(depth):
        wau = jnp.pad(w_alpha_up[l], ((0, RANK_PAD - RANK), (0, 0))).astype(BF16)
        x = _mixer(x, row(norm_mix_pre[l]), _pack_w_in(w_in[l]), wau, row(b_alpha[l]),
                   row(gla_norm[l]), w_gla_out[l].astype(BF16), conv_w[l], row(conv_b[l]),
                   row(conv_ln_g[l]), row(conv_ln_b[l]), w_conv_out[l].astype(BF16),
                   w_out[l].astype(BF16), row(norm_mix_post[l]))
        x = _ffn(x.reshape(B * S, D), row(norm_ffn_pre[l]), w_ffn_in[l].astype(BF16),
                 w_ffn_out[l].astype(BF16), row(norm_ffn_post[l])).reshape(B, S, D)
    return x
```
